```python
import math
import jax, jax.numpy as jnp
from jax import lax
import numpy as np


D_MODEL = 2048
BATCH = 8
SEQ = 4096
DEPTH = 2

N_MIXERS = 2
N_SSD_LAYERS = (DEPTH + 1) // 2
N_RET_LAYERS = DEPTH // 2
CHUNK = 128

SSD_EXPAND = 2
SSD_D_INNER = SSD_EXPAND * D_MODEL
SSD_HEAD_DIM = 64
SSD_N_HEADS = SSD_D_INNER // SSD_HEAD_DIM
SSD_N_GROUPS = 8
SSD_HEADS_PER_GROUP = SSD_N_HEADS // SSD_N_GROUPS
SSD_D_STATE = 128
SSD_CONV = 4
SSD_BC_DIM = SSD_N_GROUPS * SSD_D_STATE
SSD_CONV_DIM = SSD_D_INNER + 2 * SSD_BC_DIM
SSD_PROJ = SSD_D_INNER + SSD_CONV_DIM + SSD_N_HEADS
SSD_NORM_EPS = 1e-5

RET_N_HEADS = 8
RET_D_QK = D_MODEL
RET_D_V = 2 * D_MODEL
RET_HEAD_QK = RET_D_QK // RET_N_HEADS
RET_HEAD_V = RET_D_V // RET_N_HEADS
RET_PROJ = 2 * RET_D_QK + 2 * RET_D_V
RET_ROPE_BASE = 10000.0
RET_GN_EPS = 1e-5

MOE_N_GROUPS = 4
MOE_EXPERTS_PER_GROUP = 8
N_EXPERTS = MOE_N_GROUPS * MOE_EXPERTS_PER_GROUP
D_EXPERT = 512
TOP_K = 2
MOE_BLOCK = 128

ALPHA = (2 * DEPTH) ** 0.25
BETA = (8 * DEPTH) ** -0.25
LN_EPS = 1e-5

kernel_name = 'hybrid_ssd_retention_hmoe_deepnorm'


def _to_chunks(t):
    b, s = t.shape[:2]
    return jnp.moveaxis(t.reshape(b, s // CHUNK, CHUNK, *t.shape[2:]), 1, 0)


def _from_chunks(t):
    nc, b, l = t.shape[:3]
    return jnp.moveaxis(t, 0, 1).reshape(b, nc * l, *t.shape[3:])


def layer_norm(x, g, b):
    xf = x.astype(jnp.float32)
    mu = jnp.mean(xf, axis=-1, keepdims=True)
    var = jnp.mean(jnp.square(xf - mu), axis=-1, keepdims=True)
    return ((xf - mu) * lax.rsqrt(var + LN_EPS) * g + b).astype(x.dtype)


def causal_depthwise_conv(u, w, bias):
    ch = u.shape[-1]
    out = lax.conv_general_dilated(
        u, w[:, None, :].astype(u.dtype), window_strides=(1,),
        padding=[(SSD_CONV - 1, 0)], dimension_numbers=('NWC', 'WIO', 'NWC'),
        feature_group_count=ch)
    return out + bias


def ssd_chunked_scan(xs, dt, a, bmat, cmat):
    out_dtype = xs.dtype
    b, s, g, hg, p = xs.shape
    n = bmat.shape[-1]
    f32 = jnp.float32
    xc, dtc, bc, cc = (_to_chunks(t.astype(f32)) for t in (xs, dt, bmat, cmat))
    causal = jnp.tril(jnp.ones((CHUNK, CHUNK), dtype=bool))[None, :, :, None, None]
    a = a.astype(f32)

    def step(state, inp):
        x_k, dt_k, b_k, c_k = inp
        cum = jnp.cumsum(dt_k * a, axis=1)
        seg = cum[:, :, None] - cum[:, None, :]
        decay = jnp.exp(jnp.where(causal, seg, -jnp.inf))
        cb = jnp.einsum('bign,bjgn->bijg', c_k, b_k)
        xdt = x_k * dt_k[..., None]
        y_intra = jnp.einsum('bijg,bijgh,bjghp->bighp', cb, decay, xdt)
        y_inter = jnp.einsum('bign,bghpn->bighp', c_k, state) * jnp.exp(cum)[..., None]
        decay_end = jnp.exp(cum[:, -1:] - cum)
        new_state = (state * jnp.exp(cum[:, -1])[..., None, None]
                     + jnp.einsum('bjgn,bjgh,bjghp->bghpn', b_k, decay_end, xdt))
        return new_state, y_intra + y_inter

    state0 = jnp.zeros((b, g, hg, p, n), f32)
    _, ys = lax.scan(step, state0, (xc, dtc, bc, cc))
    return _from_chunks(ys).astype(out_dtype)


def gated_rmsnorm(y, z, g):
    u = (y * jax.nn.silu(z)).astype(jnp.float32)
    b, s, d = u.shape
    ug = u.reshape(b, s, SSD_N_GROUPS, d // SSD_N_GROUPS)
    ug = ug * lax.rsqrt(jnp.mean(ug * ug, axis=-1, keepdims=True) + SSD_NORM_EPS)
    return (ug.reshape(b, s, d) * g).astype(y.dtype)


def ssd_mixer(x, w_in, conv_w, conv_b, dt_bias, a_log, d_skip, norm_g, w_out):
    b, s, _ = x.shape
    proj = x @ w_in
    z, xbc, dt = jnp.split(proj, [SSD_D_INNER, SSD_D_INNER + SSD_CONV_DIM], axis=-1)
    xbc = jax.nn.silu(causal_depthwise_conv(xbc, conv_w, conv_b))
    xs, bmat, cmat = jnp.split(xbc, [SSD_D_INNER, SSD_D_INNER + SSD_BC_DIM], axis=-1)
    xs = xs.reshape(b, s, SSD_N_GROUPS, SSD_HEADS_PER_GROUP, SSD_HEAD_DIM)
    bmat = bmat.reshape(b, s, SSD_N_GROUPS, SSD_D_STATE)
    cmat = cmat.reshape(b, s, SSD_N_GROUPS, SSD_D_STATE)
    dt = jax.nn.softplus(dt.astype(jnp.float32) + dt_bias.astype(jnp.float32))
    dt = dt.reshape(b, s, SSD_N_GROUPS, SSD_HEADS_PER_GROUP)
    a = -jnp.exp(a_log.astype(jnp.float32)).reshape(SSD_N_GROUPS, SSD_HEADS_PER_GROUP)
    y = ssd_chunked_scan(xs, dt, a, bmat, cmat)
    y = y + d_skip.reshape(SSD_N_GROUPS, SSD_HEADS_PER_GROUP)[:, :, None] * xs
    y = gated_rmsnorm(y.reshape(b, s, SSD_D_INNER), z, norm_g)
    return y @ w_out


def xpos_rotate(t, positions):
    half = t.shape[-1] // 2
    inv_freq = 1.0 / (RET_ROPE_BASE ** jnp.linspace(0.0, 1.0, half, dtype=jnp.float32))
    ang = positions.astype(jnp.float32)[..., None] * inv_freq
    cos = jnp.cos(ang)[:, :, None, :]
    sin = jnp.sin(ang)[:, :, None, :]
    t1 = t[..., :half].astype(jnp.float32)
    t2 = t[..., half:].astype(jnp.float32)
    return jnp.concatenate([t1 * cos - t2 * sin, t1 * sin + t2 * cos], axis=-1).astype(t.dtype)


def retention_chunked(q, k, v):
    out_dtype = v.dtype
    f32 = jnp.float32
    b, s, h, dk = q.shape
    dv = v.shape[-1]
    log_g = jnp.log1p(-jnp.exp2(-5.0 - jnp.arange(h, dtype=f32)))
    idx = jnp.arange(CHUNK, dtype=f32)
    rel = idx[:, None] - idx[None, :]
    d_intra = jnp.where(rel[..., None] >= 0,
                        jnp.exp(jnp.maximum(rel, 0.0)[..., None] * log_g), 0.0)
    xi = jnp.exp((idx + 1.0)[:, None] * log_g)
    zeta = jnp.exp((CHUNK - 1.0 - idx)[:, None] * log_g)
    g_chunk = jnp.exp(CHUNK * log_g)
    qc, kc, vc = (_to_chunks(t.astype(f32)) for t in (q, k, v))

    def step(state, inp):
        q_k, k_k, v_k = inp
        scores = jnp.einsum('bihd,bjhd->bijh', q_k, k_k) * d_intra
        y_in = jnp.einsum('bijh,bjhe->bihe', scores, v_k)
        y_cross = jnp.einsum('bihd,bhde->bihe', q_k, state) * xi[None, :, :, None]
        new_state = (state * g_chunk[:, None, None]
                     + jnp.einsum('bjhd,jh,bjhe->bhde', k_k, zeta, v_k))
        return new_state, y_in + y_cross

    state0 = jnp.zeros((b, h, dk, dv), f32)
    _, ys = lax.scan(step, state0, (qc, kc, vc))
    return _from_chunks(ys).astype(out_dtype)


def retention_mixer(x, positions, w_in, gn_g, gn_b, w_out):
    b, s, _ = x.shape
    proj = x @ w_in
    q, k, v, gate = jnp.split(proj, [RET_D_QK, 2 * RET_D_QK, 2 * RET_D_QK + RET_D_V], axis=-1)
    q = xpos_rotate(q.reshape(b, s, RET_N_HEADS, RET_HEAD_QK), positions)
    k = xpos_rotate(k.reshape(b, s, RET_N_HEADS, RET_HEAD_QK), positions) * (RET_HEAD_QK ** -0.5)
    v = v.reshape(b, s, RET_N_HEADS, RET_HEAD_V)
    o = retention_chunked(q, k, v).astype(jnp.float32)
    mu = jnp.mean(o, axis=-1, keepdims=True)
    var = jnp.mean(jnp.square(o - mu), axis=-1, keepdims=True)
    o = ((o - mu) * lax.rsqrt(var + RET_GN_EPS)).reshape(b, s, RET_D_V) * gn_g + gn_b
    o = jax.nn.silu(gate) * o.astype(x.dtype)
    return o @ w_out


def hierarchical_moe(x, w_group, w_expert, w1, w3, w2):
    b, s, d = x.shape
    t = x.reshape(-1, d)
    n_tok = t.shape[0]
    group_prob = jax.nn.softmax((t @ w_group).astype(jnp.float32), axis=-1)
    p_group, g_top = lax.top_k(group_prob, 1)
    expert_logits = (t @ w_expert).astype(jnp.float32).reshape(n_tok, MOE_N_GROUPS, MOE_EXPERTS_PER_GROUP)
    in_group = jnp.take_along_axis(expert_logits, g_top[:, :, None], axis=1)[:, 0]
    p_in, e_top = lax.top_k(jax.nn.softmax(in_group, axis=-1), TOP_K)
    p_in = p_in / jnp.sum(p_in, axis=-1, keepdims=True)
    expert_id = g_top * MOE_EXPERTS_PER_GROUP + e_top
    weights = p_group * p_in
    combine = jnp.sum(jax.nn.one_hot(expert_id, N_EXPERTS, dtype=jnp.float32) * weights[..., None], axis=1)
    xb = t.reshape(-1, MOE_BLOCK, d)
    cbk = combine.reshape(-1, MOE_BLOCK, N_EXPERTS)

    def expert_block(args):
        xt, ct = args
        hdn = jax.nn.silu(jnp.einsum('td,edf->tef', xt, w1)) * jnp.einsum('td,edf->tef', xt, w3)
        hdn = hdn * ct.astype(hdn.dtype)[..., None]
        return jnp.einsum('tef,efd->td', hdn, w2)

    y = lax.map(expert_block, (xb, cbk))
    return y.reshape(b, s, d)


def setup_inputs(seed: int = 0) -> dict:
    key = jax.random.key(seed)
    ks = jax.random.split(key, 24)
    f32 = jnp.float32

    def nrm(k, shape, scale):
        return jax.random.normal(k, shape, f32) * scale

    x = nrm(ks[0], (BATCH, SEQ, D_MODEL), 1.0)
    positions = jnp.broadcast_to(jnp.arange(SEQ, dtype=jnp.int32), (BATCH, SEQ))
    ssd_col_scale = jnp.concatenate([
        jnp.ones((SSD_D_INNER,), f32), jnp.full((SSD_D_INNER,), BETA, f32),
        jnp.ones((2 * SSD_BC_DIM + SSD_N_HEADS,), f32)])
    ssd_w_in = nrm(ks[1], (N_SSD_LAYERS, D_MODEL, SSD_PROJ), D_MODEL ** -0.5) * ssd_col_scale
    ssd_conv_w = nrm(ks[2], (N_SSD_LAYERS, SSD_CONV, SSD_CONV_DIM), SSD_CONV ** -0.5)
    ssd_conv_b = nrm(ks[3], (N_SSD_LAYERS, SSD_CONV_DIM), 0.02)
    dt0 = jnp.exp(jax.random.uniform(ks[4], (N_SSD_LAYERS, SSD_N_HEADS), f32,
                                     math.log(1e-3), math.log(1e-1)))
    ssd_dt_bias = dt0 + jnp.log(-jnp.expm1(-dt0))
    ssd_a_log = jnp.log(jax.random.uniform(ks[5], (N_SSD_LAYERS, SSD_N_HEADS), f32, 1.0, 16.0))
    ssd_d = 1.0 + nrm(ks[6], (N_SSD_LAYERS, SSD_N_HEADS), 0.1)
    ssd_norm_g = 1.0 + nrm(ks[7], (N_SSD_LAYERS, SSD_D_INNER), 0.1)
    ssd_w_out = nrm(ks[8], (N_SSD_LAYERS, SSD_D_INNER, D_MODEL), SSD_D_INNER ** -0.5 * BETA)
    ret_col_scale = jnp.concatenate([
        jnp.ones((2 * RET_D_QK,), f32), jnp.full((RET_D_V,), BETA, f32), jnp.ones((RET_D_V,), f32)])
    ret_w_in = nrm(ks[9], (N_RET_LAYERS, D_MODEL, RET_PROJ), D_MODEL ** -0.5) * ret_col_scale
    ret_gn_g = 1.0 + nrm(ks[10], (N_RET_LAYERS, RET_D_V), 0.1)
    ret_gn_b = nrm(ks[11], (N_RET_LAYERS, RET_D_V), 0.02)
    ret_w_out = nrm(ks[12], (N_RET_LAYERS, RET_D_V, D_MODEL), RET_D_V ** -0.5 * BETA)
    ln1_g = 1.0 + nrm(ks[13], (DEPTH, D_MODEL), 0.1)
    ln1_b = nrm(ks[14], (DEPTH, D_MODEL), 0.02)
    ln2_g = 1.0 + nrm(ks[15], (DEPTH, D_MODEL), 0.1)
    ln2_b = nrm(ks[16], (DEPTH, D_MODEL), 0.02)
    moe_w_group = nrm(ks[17], (DEPTH, D_MODEL, MOE_N_GROUPS), D_MODEL ** -0.5)
    moe_w_expert = nrm(ks[18], (DEPTH, D_MODEL, N_EXPERTS), D_MODEL ** -0.5)
    moe_w1 = nrm(ks[19], (DEPTH, N_EXPERTS, D_MODEL, D_EXPERT), D_MODEL ** -0.5)
    moe_w3 = nrm(ks[20], (DEPTH, N_EXPERTS, D_MODEL, D_EXPERT), D_MODEL ** -0.5)
    moe_w2 = nrm(ks[21], (DEPTH, N_EXPERTS, D_EXPERT, D_MODEL), D_EXPERT ** -0.5 * BETA)
    return {'x': x, 'positions': positions,
            'ssd_w_in': ssd_w_in, 'ssd_conv_w': ssd_conv_w, 'ssd_conv_b': ssd_conv_b,
            'ssd_dt_bias': ssd_dt_bias, 'ssd_a_log': ssd_a_log, 'ssd_d': ssd_d,
            'ssd_norm_g': ssd_norm_g, 'ssd_w_out': ssd_w_out,
            'ret_w_in': ret_w_in, 'ret_gn_g': ret_gn_g, 'ret_gn_b': ret_gn_b, 'ret_w_out': ret_w_out,
            'ln1_g': ln1_g, 'ln1_b': ln1_b, 'ln2_g': ln2_g, 'ln2_b': ln2_b,
            'moe_w_group': moe_w_group, 'moe_w_expert': moe_w_expert,
            'moe_w1': moe_w1, 'moe_w3': moe_w3, 'moe_w2': moe_w2}


def reference(x, positions, ssd_w_in, ssd_conv_w, ssd_conv_b, ssd_dt_bias, ssd_a_log, ssd_d,
              ssd_norm_g, ssd_w_out, ret_w_in, ret_gn_g, ret_gn_b, ret_w_out,
              ln1_g, ln1_b, ln2_g, ln2_b, moe_w_group, moe_w_expert, moe_w1, moe_w3, moe_w2):
    for i in range(DEPTH):
        j = i // N_MIXERS
        if i % N_MIXERS == 0:
            h = ssd_mixer(x, ssd_w_in[j], ssd_conv_w[j], ssd_conv_b[j], ssd_dt_bias[j],
                          ssd_a_log[j], ssd_d[j], ssd_norm_g[j], ssd_w_out[j])
        else:
            h = retention_mixer(x, positions, ret_w_in[j], ret_gn_g[j], ret_gn_b[j], ret_w_out[j])
        x = layer_norm(ALPHA * x + h, ln1_g[i], ln1_b[i])
        f = hierarchical_moe(x, moe_w_group[i], moe_w_expert[i], moe_w1[i], moe_w3[i], moe_w2[i])
        x = layer_norm(ALPHA * x + f, ln2_g[i], ln2_b[i])
    return x
```

```python
import functools
import math

import jax
import jax.numpy as jnp
from jax import lax
from jax.experimental import pallas as pl
from jax.experimental.pallas import tpu as pltpu

F32 = jnp.float32
BF16 = jnp.bfloat16
HIGHEST = lax.Precision.HIGHEST

D_MODEL = 2048
CHUNK = 128

SSD_D_INNER = 4096
SSD_HEAD_DIM = 64
SSD_N_HEADS = 64
SSD_N_GROUPS = 8
SSD_GROUP_WIDTH = SSD_D_INNER // SSD_N_GROUPS
SSD_D_STATE = 128
SSD_CONV = 4
SSD_BC_DIM = SSD_N_GROUPS * SSD_D_STATE
SSD_CONV_DIM = SSD_D_INNER + 2 * SSD_BC_DIM
SSD_MAIN = SSD_D_INNER + SSD_CONV_DIM
SSD_NORM_EPS = 1e-5
HEAD_PAD = 128

RET_N_HEADS = 8
RET_D_QK = 2048
RET_D_V = 4096
RET_HEAD_QK = RET_D_QK // RET_N_HEADS
RET_HEAD_V = RET_D_V // RET_N_HEADS
RET_PROJ = 2 * RET_D_QK + 2 * RET_D_V
RET_ROPE_BASE = 10000.0
RET_GN_EPS = 1e-5
RET_CHUNK = 256
RET_ROWS = 512

MOE_N_GROUPS = 4
MOE_EXPERTS_PER_GROUP = 8
N_EXPERTS = MOE_N_GROUPS * MOE_EXPERTS_PER_GROUP
D_EXPERT = 512
TOP_K = 2
ROUTER_ROWS = 8 + N_EXPERTS
MOE_TILE = 256

DEPTH = 2
ALPHA = (2 * DEPTH) ** 0.25
LN_EPS = 1e-5

VMEM_LIMIT_BYTES = 56 * 1024 * 1024


def _params(*sem):
    return pltpu.CompilerParams(dimension_semantics=sem, vmem_limit_bytes=VMEM_LIMIT_BYTES)


def _silu(v):
    return v * jax.nn.sigmoid(v)


def _softplus(v):
    return jnp.maximum(v, 0.0) + jnp.log1p(jnp.exp(-jnp.abs(v)))


def _layer_norm(v, g, b):
    mu = jnp.mean(v, axis=-1, keepdims=True)
    d = v - mu
    var = jnp.mean(d * d, axis=-1, keepdims=True)
    return d * lax.rsqrt(var + LN_EPS) * g + b


def _dot(a, b):
    return jnp.dot(a, b, preferred_element_type=F32)


def _dot_nt(a, b):
    return lax.dot_general(a, b, (((1,), (1,)), ((), ())), preferred_element_type=F32)


def _dot_tn(a, b):
    return lax.dot_general(a, b, (((0,), (0,)), ((), ())), preferred_element_type=F32)


def _inproj_kernel(x_ref, w_ref, o_ref):
    o_ref[...] = _dot(x_ref[...].astype(BF16), w_ref[...]).astype(o_ref.dtype)


def _inproj(x, w, tm, tn):
    t, k = x.shape
    n = w.shape[1]
    return pl.pallas_call(
        _inproj_kernel,
        grid=(t // tm, n // tn),
        in_specs=[pl.BlockSpec((tm, k), lambda i, j: (i, 0)),
                  pl.BlockSpec((k, tn), lambda i, j: (0, j))],
        out_specs=pl.BlockSpec((tm, tn), lambda i, j: (i, j)),
        out_shape=jax.ShapeDtypeStruct((t, n), BF16),
        compiler_params=_params("parallel", "arbitrary"),
        name="inproj",
    )(x, w)


def _inproj_ssd_kernel(x_ref, w_ref, wdt_ref, wdt_t_ref, b_ref, b_t_ref, o_ref, dt_ref, dt_t_ref):
    xb = x_ref[...].astype(BF16)
    o_ref[...] = _dot(xb, w_ref[...]).astype(o_ref.dtype)

    @pl.when(pl.program_id(1) == 0)
    def _():
        dt_ref[...] = _softplus(_dot(xb, wdt_ref[...]) + b_ref[...])
        dt_t_ref[...] = _softplus(_dot_nt(wdt_t_ref[...], xb) + b_t_ref[...])


def _inproj_ssd(x, w, wdt, wdt_t, b, b_t, tm, tn):
    t, k = x.shape
    n = w.shape[1]
    const = lambda i, j: (0, 0)
    return pl.pallas_call(
        _inproj_ssd_kernel,
        grid=(t // tm, n // tn),
        in_specs=[pl.BlockSpec((tm, k), lambda i, j: (i, 0)),
                  pl.BlockSpec((k, tn), lambda i, j: (0, j)),
                  pl.BlockSpec((k, HEAD_PAD), const),
                  pl.BlockSpec((HEAD_PAD, k), const),
                  pl.BlockSpec((1, HEAD_PAD), const),
                  pl.BlockSpec((HEAD_PAD, 1), const)],
        out_specs=[pl.BlockSpec((tm, tn), lambda i, j: (i, j)),
                   pl.BlockSpec((tm, HEAD_PAD), lambda i, j: (i, 0)),
                   pl.BlockSpec((HEAD_PAD, tm), lambda i, j: (0, i))],
        out_shape=[jax.ShapeDtypeStruct((t, n), BF16),
                   jax.ShapeDtypeStruct((t, HEAD_PAD), F32),
                   jax.ShapeDtypeStruct((HEAD_PAD, t), F32)],
        compiler_params=_params("parallel", "arbitrary"),
        name="inproj_ssd",
    )(x, w, wdt, wdt_t, b, b_t)


def _ssd_kernel(z_ref, xs_ref, bc_ref, dt_ref, dt_t_ref, cw_ref, cb_ref, alog_ref, alog_t_ref,
                dsk_ref, ng_ref, o_ref, ubuf, act, state):
    L = CHUNK
    gw = SSD_GROUP_WIDTH
    tail = 8

    @pl.when(pl.program_id(1) == 0)
    def _():
        ubuf[0:tail, :] = jnp.zeros((tail, SSD_CONV_DIM), F32)
        state[...] = jnp.zeros(state.shape, F32)

    ubuf[tail:tail + L, 0:SSD_D_INNER] = xs_ref[...].astype(F32)
    ubuf[tail:tail + L, SSD_D_INNER:SSD_CONV_DIM] = bc_ref[...].astype(F32)
    for j in range(SSD_CONV_DIM // gw):
        cs = slice(j * gw, (j + 1) * gw)
        acc = cb_ref[:, cs]
        for k in range(SSD_CONV):
            r0 = tail - (SSD_CONV - 1) + k
            acc = acc + ubuf[r0:r0 + L, cs] * cw_ref[k:k + 1, cs]
        act[:, cs] = _silu(acc)
    ubuf[0:tail, :] = ubuf[L:L + tail, :]

    dt = dt_ref[...]
    da = dt * (-jnp.exp(alog_ref[...]))
    da_t = dt_t_ref[...] * (-jnp.exp(alog_t_ref[...]))
    row = lax.broadcasted_iota(jnp.int32, (L, L), 0)
    col = lax.broadcasted_iota(jnp.int32, (L, L), 1)
    causal = row >= col
    lower = causal.astype(F32)
    upper = (row <= col).astype(F32)
    cum = jnp.dot(lower, da, precision=HIGHEST, preferred_element_type=F32)
    cum_t = jnp.dot(da_t, upper, precision=HIGHEST, preferred_element_type=F32)
    ecum = jnp.exp(cum)
    cum_last = cum[L - 1:L, :]
    e_last = jnp.exp(cum_last)
    dt_end = dt * jnp.exp(cum_last - cum)

    lane_lo = lax.broadcasted_iota(jnp.int32, (L, 2 * SSD_HEAD_DIM), 1) < SSD_HEAD_DIM
    lane_lo_row = lane_lo[0:1, :]

    def pair_cols(v, h0):
        return jnp.where(lane_lo, v[:, h0:h0 + 1], v[:, h0 + 1:h0 + 2])

    def pair_row(v, h0):
        return jnp.where(lane_lo_row, v[:, h0:h0 + 1], v[:, h0 + 1:h0 + 2])

    neg_inf = jnp.float32(-jnp.inf)
    for g in range(SSD_N_GROUPS):
        b_off = SSD_D_INNER + g * SSD_D_STATE
        c_off = SSD_D_INNER + SSD_BC_DIM + g * SSD_D_STATE
        b_g = act[:, b_off:b_off + SSD_D_STATE]
        c_g = act[:, c_off:c_off + SSD_D_STATE].astype(BF16)
        cb = _dot_nt(c_g, b_g.astype(BF16))
        s_old = state[g]
        y_inter = _dot(c_g, s_old.astype(BF16))
        xdt_end = []
        e_last_row = []
        y_parts = []
        for c in range(gw // (2 * SSD_HEAD_DIM)):
            h0 = g * (gw // SSD_HEAD_DIM) + 2 * c
            ls = slice(g * gw + c * 2 * SSD_HEAD_DIM, g * gw + (c + 1) * 2 * SSD_HEAD_DIM)
            xs_p = act[:, ls]
            xdt = xs_p * pair_cols(dt, h0)
            m = []
            for h in (h0, h0 + 1):
                seg = cum[:, h:h + 1] - cum_t[h:h + 1, :]
                m.append((cb * jnp.exp(jnp.where(causal, seg, neg_inf))).astype(BF16))
            lhs = jnp.concatenate(m, axis=1)
            rhs = jnp.concatenate([jnp.where(lane_lo, xdt, 0.0), jnp.where(lane_lo, 0.0, xdt)],
                                  axis=0).astype(BF16)
            y = _dot(lhs, rhs)
            y = y + y_inter[:, c * 2 * SSD_HEAD_DIM:(c + 1) * 2 * SSD_HEAD_DIM] * pair_cols(ecum, h0)
            y = y + pair_row(dsk_ref[...], h0) * xs_p
            y_parts.append(y)
            xdt_end.append(xs_p * pair_cols(dt_end, h0))
            e_last_row.append(pair_row(e_last, h0))
        xdt_end = jnp.concatenate(xdt_end, axis=1).astype(BF16)
        e_last_row = jnp.concatenate(e_last_row, axis=1)
        state[g] = s_old * e_last_row + _dot(b_g.T.astype(BF16), xdt_end)
        y_g = jnp.concatenate(y_parts, axis=1)
        gs = slice(g * gw, (g + 1) * gw)
        u = y_g * _silu(z_ref[:, gs].astype(F32))
        u = u * lax.rsqrt(jnp.mean(u * u, axis=-1, keepdims=True) + SSD_NORM_EPS)
        o_ref[:, gs] = (u * ng_ref[:, gs]).astype(o_ref.dtype)


def _ssd_mix(proj, dt, dt_t, conv_w, conv_b, alog, alog_t, dsk, norm_g, batch, seq):
    t = proj.shape[0]
    nc = seq // CHUNK
    rows = lambda b, c: b * nc + c
    const = lambda b, c: (0, 0)
    return pl.pallas_call(
        _ssd_kernel,
        grid=(batch, nc),
        in_specs=[pl.BlockSpec((CHUNK, SSD_D_INNER), lambda b, c: (rows(b, c), 0)),
                  pl.BlockSpec((CHUNK, SSD_D_INNER), lambda b, c: (rows(b, c), 1)),
                  pl.BlockSpec((CHUNK, 2 * SSD_BC_DIM),
                               lambda b, c: (rows(b, c), 2 * SSD_D_INNER // (2 * SSD_BC_DIM))),
                  pl.BlockSpec((CHUNK, HEAD_PAD), lambda b, c: (rows(b, c), 0)),
                  pl.BlockSpec((HEAD_PAD, CHUNK), lambda b, c: (0, rows(b, c))),
                  pl.BlockSpec((SSD_CONV, SSD_CONV_DIM), const),
                  pl.BlockSpec((1, SSD_CONV_DIM), const),
                  pl.BlockSpec((1, HEAD_PAD), const),
                  pl.BlockSpec((HEAD_PAD, 1), const),
                  pl.BlockSpec((1, HEAD_PAD), const),
                  pl.BlockSpec((1, SSD_D_INNER), const)],
        out_specs=pl.BlockSpec((CHUNK, SSD_D_INNER), lambda b, c: (rows(b, c), 0)),
        out_shape=jax.ShapeDtypeStruct((t, SSD_D_INNER), BF16),
        scratch_shapes=[pltpu.VMEM((CHUNK + 8, SSD_CONV_DIM), F32),
                        pltpu.VMEM((CHUNK, SSD_CONV_DIM), F32),
                        pltpu.VMEM((SSD_N_GROUPS, SSD_D_STATE, SSD_GROUP_WIDTH), F32)],
        compiler_params=_params("parallel", "arbitrary"),
        name="ssd_mix",
    )(proj, proj, proj, dt, dt_t, conv_w, conv_b, alog, alog_t, dsk, norm_g)


def _rope_kernel(pos_ref, invf_ref, cos_ref, sin_ref):
    ang = pos_ref[...].astype(F32) * invf_ref[...]
    cos_ref[...] = jnp.cos(ang)
    sin_ref[...] = jnp.sin(ang)


def _rope_tables(pos_col, inv_freq, tr):
    t = pos_col.shape[0]
    half = inv_freq.shape[1]
    return pl.pallas_call(
        _rope_kernel,
        grid=(t // tr,),
        in_specs=[pl.BlockSpec((tr, 1), lambda i: (i, 0)),
                  pl.BlockSpec((1, half), lambda i: (0, 0))],
        out_specs=[pl.BlockSpec((tr, half), lambda i: (i, 0))] * 2,
        out_shape=[jax.ShapeDtypeStruct((t, half), F32)] * 2,
        compiler_params=_params("parallel"),
        name="rope_tables",
    )(pos_col, inv_freq)


def _ret_kernel(lg_ref, q_ref, k_ref, v_ref, gate_ref, cos_ref, sin_ref, gg_ref, gb_ref, o_ref,
                state):
    L = RET_CHUNK
    half = RET_HEAD_QK // 2

    @pl.when(pl.program_id(2) == 0)
    def _():
        state[...] = jnp.zeros(state.shape, F32)

    log_g = lg_ref[pl.program_id(1)]
    rel = (lax.broadcasted_iota(jnp.int32, (L, L), 0)
           - lax.broadcasted_iota(jnp.int32, (L, L), 1)).astype(F32)
    d_intra = jnp.where(rel >= 0.0, jnp.exp(jnp.maximum(rel, 0.0) * log_g), 0.0)
    idx = lax.broadcasted_iota(jnp.int32, (L, 1), 0).astype(F32)
    xi = jnp.exp((idx + 1.0) * log_g)
    zeta = jnp.exp((L - 1.0 - idx) * log_g)
    g_chunk = jnp.exp(jnp.full((1, 1), L, F32) * log_g)

    def rotate(t, cos, sin):
        t1, t2 = t[:, :half], t[:, half:]
        return jnp.concatenate([t1 * cos - t2 * sin, t1 * sin + t2 * cos], axis=1)

    for s in range(q_ref.shape[0] // L):
        rs = slice(s * L, (s + 1) * L)
        cos, sin = cos_ref[rs, :], sin_ref[rs, :]
        q = rotate(q_ref[rs, :].astype(F32), cos, sin)
        k = rotate(k_ref[rs, :].astype(F32), cos, sin) * (RET_HEAD_QK ** -0.5)
        v = v_ref[rs, :]
        qb = q.astype(BF16)
        scores = _dot_nt(qb, k.astype(BF16)) * d_intra
        s_old = state[...]
        y = _dot(scores.astype(BF16), v) + _dot(qb, s_old.astype(BF16)) * xi
        state[...] = s_old * g_chunk + _dot_tn((k * zeta).astype(BF16), v)
        mu = jnp.mean(y, axis=-1, keepdims=True)
        d = y - mu
        var = jnp.mean(d * d, axis=-1, keepdims=True)
        o = d * lax.rsqrt(var + RET_GN_EPS) * gg_ref[...] + gb_ref[...]
        o_ref[rs, :] = (_silu(gate_ref[rs, :].astype(F32)) * o).astype(o_ref.dtype)


def _ret_mix(log_g, proj, cos, sin, gn_g, gn_b, batch, seq, rows):
    t = proj.shape[0]
    nb = seq // rows
    r = lambda b, c: b * nb + c
    qk_blocks = RET_D_QK // RET_HEAD_QK
    v_blocks = RET_D_V // RET_HEAD_V
    grid_spec = pltpu.PrefetchScalarGridSpec(
        num_scalar_prefetch=1,
        grid=(batch, RET_N_HEADS, nb),
        in_specs=[pl.BlockSpec((rows, RET_HEAD_QK), lambda b, h, c, lg: (r(b, c), h)),
                  pl.BlockSpec((rows, RET_HEAD_QK), lambda b, h, c, lg: (r(b, c), qk_blocks + h)),
                  pl.BlockSpec((rows, RET_HEAD_V),
                               lambda b, h, c, lg: (r(b, c), 2 * RET_D_QK // RET_HEAD_V + h)),
                  pl.BlockSpec((rows, RET_HEAD_V),
                               lambda b, h, c, lg: (r(b, c), 2 * RET_D_QK // RET_HEAD_V + v_blocks + h)),
                  pl.BlockSpec((rows, RET_HEAD_QK // 2), lambda b, h, c, lg: (r(b, c), 0)),
                  pl.BlockSpec((rows, RET_HEAD_QK // 2), lambda b, h, c, lg: (r(b, c), 0)),
                  pl.BlockSpec((1, RET_HEAD_V), lambda b, h, c, lg: (0, h)),
                  pl.BlockSpec((1, RET_HEAD_V), lambda b, h, c, lg: (0, h))],
        out_specs=pl.BlockSpec((rows, RET_HEAD_V), lambda b, h, c, lg: (r(b, c), h)),
        scratch_shapes=[pltpu.VMEM((RET_HEAD_QK, RET_HEAD_V), F32)],
    )
    return pl.pallas_call(
        _ret_kernel,
        grid_spec=grid_spec,
        out_shape=jax.ShapeDtypeStruct((t, RET_D_V), BF16),
        compiler_params=_params("parallel", "parallel", "arbitrary"),
        name="ret_mix",
    )(log_g, proj, proj, proj, proj, cos, sin, gn_g, gn_b)


def _outproj_ln_kernel(h_ref, w_ref, x_ref, g_ref, b_ref, o_ref):
    v = ALPHA * x_ref[...] + _dot(h_ref[...], w_ref[...])
    o_ref[...] = _layer_norm(v, g_ref[...], b_ref[...])


def _outproj_ln(h, w, x, g, b, tm):
    t, k = h.shape
    n = w.shape[1]
    const = lambda i: (0, 0)
    return pl.pallas_call(
        _outproj_ln_kernel,
        grid=(t // tm,),
        in_specs=[pl.BlockSpec((tm, k), lambda i: (i, 0)),
                  pl.BlockSpec((k, n), const, pipeline_mode=pl.Buffered(1)),
                  pl.BlockSpec((tm, n), lambda i: (i, 0)),
                  pl.BlockSpec((1, n), const),
                  pl.BlockSpec((1, n), const)],
        out_specs=pl.BlockSpec((tm, n), lambda i: (i, 0)),
        out_shape=jax.ShapeDtypeStruct((t, n), F32),
        compiler_params=_params("parallel"),
        name="outproj_ln",
    )(h, w, x, g, b)


def _router_kernel(x_ref, w_ref, eid_ref, wt_ref):
    logits = lax.dot_general(w_ref[...], x_ref[...], (((1,), (1,)), ((), ())),
                             precision=HIGHEST, preferred_element_type=F32)
    epg = MOE_EXPERTS_PER_GROUP
    tr = logits.shape[1]
    row = lax.broadcasted_iota(jnp.int32, (epg, tr), 0)
    neg_inf = jnp.float32(-jnp.inf)

    def first_max(v):
        m = jnp.max(v, axis=0, keepdims=True)
        return m, jnp.min(jnp.where(v == m, row, epg), axis=0, keepdims=True)

    gl = jnp.where(row < MOE_N_GROUPS, logits[0:epg, :], neg_inf)
    g_max, g_top = first_max(gl)
    p_group = 1.0 / jnp.sum(jnp.exp(gl - g_max), axis=0, keepdims=True)

    sel = jnp.zeros((epg, tr), F32)
    for g in range(MOE_N_GROUPS):
        sel = jnp.where(g_top == g, logits[epg * (1 + g):epg * (2 + g), :], sel)
    e = jnp.exp(sel - jnp.max(sel, axis=0, keepdims=True))
    p = e / jnp.sum(e, axis=0, keepdims=True)
    p1, i1 = first_max(p)
    p2, i2 = first_max(jnp.where(row == i1, -1.0, p))
    den = p1 + p2
    eid_ref[0:1, :] = g_top * epg + i1
    eid_ref[1:2, :] = g_top * epg + i2
    wt_ref[0:1, :] = p_group * (p1 / den)
    wt_ref[1:2, :] = p_group * (p2 / den)


def _router(x, w_rt, tr):
    t, d = x.shape
    return pl.pallas_call(
        _router_kernel,
        grid=(t // tr,),
        in_specs=[pl.BlockSpec((tr, d), lambda i: (i, 0)),
                  pl.BlockSpec((ROUTER_ROWS, d), lambda i: (0, 0))],
        out_specs=[pl.BlockSpec((TOP_K, tr), lambda i: (0, i))] * 2,
        out_shape=[jax.ShapeDtypeStruct((TOP_K, t), jnp.int32),
                   jax.ShapeDtypeStruct((TOP_K, t), F32)],
        compiler_params=_params("parallel"),
        name="router",
    )(x, w_rt)


def _positions(eid, n_tiles):
    flat = eid.reshape(-1)
    oh = jax.nn.one_hot(flat, N_EXPERTS, dtype=jnp.int32)
    csum = jnp.cumsum(oh, axis=0)
    rank = jnp.take_along_axis(csum, flat[:, None], axis=1)[:, 0] - 1
    counts = csum[-1]
    padded = (counts + MOE_TILE - 1) // MOE_TILE * MOE_TILE
    end = jnp.cumsum(padded)
    base = end - padded
    pos = (base[flat] + rank).astype(jnp.int32)
    tile_start = jnp.arange(n_tiles, dtype=jnp.int32) * MOE_TILE
    tile_expert = jnp.sum((end[None, :] <= tile_start[:, None]).astype(jnp.int32), axis=1)
    tile_expert = jnp.minimum(tile_expert, N_EXPERTS - 1).astype(jnp.int32)
    n_used = (end[-1:] // MOE_TILE).astype(jnp.int32)
    zero_start = jnp.where(padded > 0, end - MOE_TILE, -1).astype(jnp.int32)
    return pos, tile_expert, n_used, zero_start


def _row_copy(src, s, dst, d, sem):
    return pltpu.make_async_copy(src.at[pl.ds(s, 1), :], dst.at[pl.ds(d, 1), :], sem)


def _dispatch_kernel(pos_ref, zs_ref, nu_ref, x_hbm, o_hbm, zbuf, sem, *, tb, n_tok):
    i = pl.program_id(0)

    @pl.when(i == 0)
    def _():
        zbuf[...] = jnp.zeros(zbuf.shape, zbuf.dtype)

        def zero_rows(start):
            start = pl.multiple_of(start, MOE_TILE)
            cp = pltpu.make_async_copy(zbuf, o_hbm.at[pl.ds(start, MOE_TILE), :], sem)
            cp.start()
            cp.wait()

        def zero_last_tile(e, carry):
            @pl.when(zs_ref[e] >= 0)
            def _():
                zero_rows(zs_ref[e])
            return carry

        lax.fori_loop(0, N_EXPERTS, zero_last_tile, 0)

        def zero_tail_tile(j, carry):
            zero_rows(j * MOE_TILE)
            return carry

        lax.fori_loop(nu_ref[0], o_hbm.shape[0] // MOE_TILE, zero_tail_tile, 0)

    base = i * tb

    def issue(t, carry):
        for k in range(TOP_K):
            _row_copy(x_hbm, base + t, o_hbm, pos_ref[k * n_tok + base + t], sem).start()
        return carry

    lax.fori_loop(0, tb, issue, 0)

    def drain(t, carry):
        for k in range(TOP_K):
            _row_copy(x_hbm, base + t, o_hbm, pos_ref[k * n_tok + base + t], sem).wait()
        return carry

    lax.fori_loop(0, tb, drain, 0)


def _dispatch(pos, zero_start, n_used, x, n_rows, tb):
    t, d = x.shape
    grid_spec = pltpu.PrefetchScalarGridSpec(
        num_scalar_prefetch=3,
        grid=(t // tb,),
        in_specs=[pl.BlockSpec(memory_space=pl.ANY)],
        out_specs=pl.BlockSpec(memory_space=pl.ANY),
        scratch_shapes=[pltpu.VMEM((MOE_TILE, d), x.dtype), pltpu.SemaphoreType.DMA],
    )
    return pl.pallas_call(
        functools.partial(_dispatch_kernel, tb=tb, n_tok=t),
        grid_spec=grid_spec,
        out_shape=jax.ShapeDtypeStruct((n_rows, d), x.dtype),
        compiler_params=_params("arbitrary"),
        name="moe_dispatch",
    )(pos, zero_start, n_used, x)


def _experts_kernel(te_ref, nu_ref, x_ref, w1_ref, w3_ref, w2_ref, o_ref):
    used = pl.program_id(0) < nu_ref[0]

    @pl.when(used)
    def _():
        xb = x_ref[...].astype(BF16)
        h1 = _dot(xb, w1_ref[0])
        h3 = _dot(xb, w3_ref[0])
        o_ref[...] = _dot((_silu(h1) * h3).astype(BF16), w2_ref[0])

    @pl.when(jnp.logical_not(used))
    def _():
        o_ref[...] = jnp.zeros(o_ref.shape, o_ref.dtype)


def _experts(tile_expert, n_used, xs, w1, w3, w2):
    n_rows, d = xs.shape
    f = w1.shape[2]
    tile = lambda i, te, nu: jnp.minimum(i, nu[0] - 1)
    grid_spec = pltpu.PrefetchScalarGridSpec(
        num_scalar_prefetch=2,
        grid=(n_rows // MOE_TILE,),
        in_specs=[pl.BlockSpec((MOE_TILE, d), lambda i, te, nu: (tile(i, te, nu), 0)),
                  pl.BlockSpec((1, d, f), lambda i, te, nu: (te[tile(i, te, nu)], 0, 0)),
                  pl.BlockSpec((1, d, f), lambda i, te, nu: (te[tile(i, te, nu)], 0, 0)),
                  pl.BlockSpec((1, f, d), lambda i, te, nu: (te[tile(i, te, nu)], 0, 0))],
        out_specs=pl.BlockSpec((MOE_TILE, d), lambda i, te, nu: (i, 0)),
    )
    return pl.pallas_call(
        _experts_kernel,
        grid_spec=grid_spec,
        out_shape=jax.ShapeDtypeStruct((n_rows, d), F32),
        compiler_params=_params("arbitrary"),
        name="moe_experts",
    )(tile_expert, n_used, xs, w1, w3, w2)


def _combine_ln_kernel(pos_ref, y_hbm, x_ref, wt_ref, g_ref, b_ref, o_ref, buf, sem, *, tb, n_tok):
    base = pl.program_id(0) * tb

    def issue(t, carry):
        for k in range(TOP_K):
            _row_copy(y_hbm, pos_ref[k * n_tok + base + t], buf.at[k], t, sem).start()
        return carry

    lax.fori_loop(0, tb, issue, 0)

    def drain(t, carry):
        for k in range(TOP_K):
            _row_copy(y_hbm, pos_ref[k * n_tok + base + t], buf.at[k], t, sem).wait()
        return carry

    lax.fori_loop(0, tb, drain, 0)

    f = wt_ref[:, 0:1] * buf[0] + wt_ref[:, 1:2] * buf[1]
    o_ref[...] = _layer_norm(ALPHA * x_ref[...] + f, g_ref[...], b_ref[...])


def _combine_ln(pos, ys, x, wt_cols, g, b, tb):
    t, d = x.shape
    const = lambda i, pos: (0, 0)
    grid_spec = pltpu.PrefetchScalarGridSpec(
        num_scalar_prefetch=1,
        grid=(t // tb,),
        in_specs=[pl.BlockSpec(memory_space=pl.ANY),
                  pl.BlockSpec((tb, d), lambda i, pos: (i, 0)),
                  pl.BlockSpec((tb, TOP_K), lambda i, pos: (i, 0)),
                  pl.BlockSpec((1, d), const),
                  pl.BlockSpec((1, d), const)],
        out_specs=pl.BlockSpec((tb, d), lambda i, pos: (i, 0)),
        scratch_shapes=[pltpu.VMEM((TOP_K, tb, d), ys.dtype), pltpu.SemaphoreType.DMA],
    )
    return pl.pallas_call(
        functools.partial(_combine_ln_kernel, tb=tb, n_tok=t),
        grid_spec=grid_spec,
        out_shape=jax.ShapeDtypeStruct((t, d), F32),
        compiler_params=_params("arbitrary"),
        name="moe_combine_ln",
    )(pos, ys, x, wt_cols, g, b)


def _moe_ln(x, w_group, w_expert, w1, w3, w2, g, b):
    t, d = x.shape
    pad_rows = jnp.zeros((ROUTER_ROWS - N_EXPERTS - MOE_N_GROUPS, d), F32)
    w_rt = jnp.concatenate([w_group.T, pad_rows, w_expert.T], axis=0)
    eid, wts = _router(x, w_rt, min(t, 512))
    n_tiles = (t * TOP_K) // MOE_TILE + N_EXPERTS
    pos, tile_expert, n_used, zero_start = _positions(eid, n_tiles)
    xs = _dispatch(pos, zero_start, n_used, x, n_tiles * MOE_TILE, min(t, 1024))
    ys = _experts(tile_expert, n_used, xs, w1.astype(BF16), w3.astype(BF16), w2.astype(BF16))
    return _combine_ln(pos, ys, x, wts.T, g[None, :], b[None, :], min(t, 256))


def _pad_heads(v, fill=0.0):
    return jnp.pad(v.astype(F32), (0, HEAD_PAD - v.shape[0]), constant_values=fill)


def _ssd_layer(x, batch, seq, w_in, conv_w, conv_b, dt_bias, a_log, d_skip, norm_g, w_out, ln_g, ln_b):
    t = x.shape[0]
    w_main = w_in[:, :SSD_MAIN].astype(BF16)
    w_dt = jnp.pad(w_in[:, SSD_MAIN:], ((0, 0), (0, HEAD_PAD - SSD_N_HEADS))).astype(BF16)
    bias = _pad_heads(dt_bias)
    alog = _pad_heads(a_log)
    proj, dt, dt_t = _inproj_ssd(x, w_main, w_dt, w_dt.T, bias[None, :], bias[:, None],
                                 min(t, 1024), 512)
    y = _ssd_mix(proj, dt, dt_t, conv_w, conv_b[None, :], alog[None, :], alog[:, None],
                 _pad_heads(d_skip)[None, :], norm_g[None, :], batch, seq)
    return _outproj_ln(y, w_out.astype(BF16), x, ln_g[None, :], ln_b[None, :], min(t, 256))


def _ret_layer(x, positions, batch, seq, w_in, gn_g, gn_b, w_out, ln_g, ln_b):
    t = x.shape[0]
    proj = _inproj(x, w_in.astype(BF16), min(t, 1024), 512)
    half = RET_HEAD_QK // 2
    inv_freq = 1.0 / (RET_ROPE_BASE ** jnp.linspace(0.0, 1.0, half, dtype=F32))
    cos, sin = _rope_tables(positions.reshape(t, 1), inv_freq[None, :], min(t, 1024))
    log_g = jnp.log1p(-jnp.exp2(-5.0 - jnp.arange(RET_N_HEADS, dtype=F32)))
    o = _ret_mix(log_g, proj, cos, sin, gn_g[None, :], gn_b[None, :], batch, seq,
                 min(seq, RET_ROWS))
    return _outproj_ln(o, w_out.astype(BF16), x, ln_g[None, :], ln_b[None, :], min(t, 256))


def kernel(x, positions, ssd_w_in, ssd_conv_w, ssd_conv_b, ssd_dt_bias, ssd_a_log, ssd_d, ssd_norm_g, ssd_w_out, ret_w_in, ret_gn_g, ret_gn_b, ret_w_out, ln1_g, ln1_b, ln2_g, ln2_b, moe_w_group, moe_w_expert, moe_w1, moe_w3, moe_w2):
    batch, seq, d = x.shape
    assert d == D_MODEL and seq % RET_CHUNK == 0 and (batch * seq * TOP_K) % MOE_TILE == 0
    h = x.reshape(batch * seq, d)
    for i in range(DEPTH):
        j = i // 2
        if i % 2 == 0:
            h = _ssd_layer(h, batch, seq, ssd_w_in[j], ssd_conv_w[j], ssd_conv_b[j], ssd_dt_bias[j],
                           ssd_a_log[j], ssd_d[j], ssd_norm_g[j], ssd_w_out[j], ln1_g[i], ln1_b[i])
        else:
            h = _ret_layer(h, positions, batch, seq, ret_w_in[j], ret_gn_g[j], ret_gn_b[j],
                           ret_w_out[j], ln1_g[i], ln1_b[i])
        h = _moe_ln(h, moe_w_group[i], moe_w_expert[i], moe_w1[i], moe_w3[i], moe_w2[i],
                    ln2_g[i], ln2_b[i])
    return h.reshape(batch, seq, d)
```
